```python
import math
import jax, jax.numpy as jnp
from jax import lax
import numpy as np

D_MODEL = 1024
BATCH = 8
SEQ = 2048
DEPTH = 2
DEC_BATCH = 16
DEC_SEQ = 16
PAST_LEN = 1024

CHUNK = 64
N_HEADS = 8
HEAD_DIM = 64
V_DIM = 2 * HEAD_DIM
W_ATTN = N_HEADS * V_DIM
QK_COLS = 2 * N_HEADS * HEAD_DIM
GROUP_CH = 16
W_SSM = 1024
N_GROUPS = W_SSM // GROUP_CH
STATE_DIM = 64
Q_BLOCK = 128
EPS = 1e-6
IN_COLS = 2 * QK_COLS + 2 * W_ATTN + 2 * W_SSM + 2 * D_MODEL
IN_SPLITS = (QK_COLS, 2 * QK_COLS, 2 * QK_COLS + W_ATTN, 2 * QK_COLS + 2 * W_ATTN,
             2 * QK_COLS + 2 * W_ATTN + W_SSM, 2 * QK_COLS + 2 * W_ATTN + 2 * W_SSM)

kernel_name = "hybrid_diffattn_s5_streaming_step"


def rmsnorm(x, g):
    x32 = x.astype(jnp.float32)
    y = x32 * lax.rsqrt(jnp.mean(x32 * x32, axis=-1, keepdims=True) + EPS)
    return (y * g.astype(jnp.float32)).astype(x.dtype)


def alibi_slopes():
    return jnp.exp2(-8.0 * (jnp.arange(N_HEADS, dtype=jnp.float32) + 1.0) / N_HEADS)


def diff_attn_core(q, k, v, q_pos, k_pos, lam, slopes):
    s = jnp.einsum('bqmhd,bkmhd->bmhqk', q, k.astype(jnp.float32))
    dist = jnp.abs(q_pos[:, None] - k_pos[None, :]).astype(jnp.float32)
    allowed = (k_pos[None, :] // CHUNK) <= (q_pos[:, None] // CHUNK)
    s = s - slopes[:, None, None] * dist
    s = jnp.where(allowed, s, -jnp.inf)
    p = jax.nn.softmax(s, axis=-1)
    w = p[:, 0] - lam * p[:, 1]
    return jnp.einsum('bhqk,bkhe->bqhe', w, v.astype(jnp.float32))


def attn_prompt(q, k, v, lam, slopes):
    b, t = q.shape[0], q.shape[1]
    nb = t // Q_BLOCK
    qb = q.reshape(b, nb, Q_BLOCK, 2, N_HEADS, HEAD_DIM).transpose(1, 0, 2, 3, 4, 5)
    k_pos = jnp.arange(t)

    def one_block(args):
        q_blk, i = args
        q_pos = i * Q_BLOCK + jnp.arange(Q_BLOCK)
        return diff_attn_core(q_blk, k, v, q_pos, k_pos, lam, slopes)

    o = lax.map(one_block, (qb, jnp.arange(nb)))
    return o.transpose(1, 0, 2, 3, 4).reshape(b, t, N_HEADS, V_DIM)


def attn_sample(q, k, v, k_past, v_past, lam, slopes):
    t = q.shape[1]
    p_len = k_past.shape[1]
    k_all = jnp.concatenate([k_past.astype(k.dtype), k], axis=1)
    v_all = jnp.concatenate([v_past.astype(v.dtype), v], axis=1)
    q_pos = p_len + jnp.arange(t)
    k_pos = jnp.arange(p_len + t)
    return diff_attn_core(q, k_all, v_all, q_pos, k_pos, lam, slopes)


def s5_scan(u, h0_re, h0_im, lam_re, lam_im, log_dt, b_re, b_im, c_re, c_im, d_skip):
    b, t, _ = u.shape
    f32 = jnp.float32
    u32 = u.astype(f32).reshape(b, t, N_GROUPS, GROUP_CH)
    lr, li = lam_re.astype(f32), lam_im.astype(f32)
    dt = jnp.exp(log_dt.astype(f32))[:, None]
    mag = jnp.exp(lr * dt)
    ang = li * dt
    a_re, a_im = mag * jnp.cos(ang), mag * jnp.sin(ang)
    z_re, z_im = a_re - 1.0, a_im
    den = lr * lr + li * li
    w_re = (z_re * lr + z_im * li) / den
    w_im = (z_im * lr - z_re * li) / den
    br, bi = b_re.astype(f32), b_im.astype(f32)
    bb_re = w_re[..., None] * br - w_im[..., None] * bi
    bb_im = w_re[..., None] * bi + w_im[..., None] * br
    bu_re = jnp.einsum('btgn,gpn->btgp', u32, bb_re)
    bu_im = jnp.einsum('btgn,gpn->btgp', u32, bb_im)
    h0r, h0i = h0_re.astype(f32), h0_im.astype(f32)
    bu_re = bu_re.at[:, 0].add(a_re * h0r - a_im * h0i)
    bu_im = bu_im.at[:, 0].add(a_re * h0i + a_im * h0r)
    a_re_t = jnp.broadcast_to(a_re, (1, t, N_GROUPS, STATE_DIM))
    a_im_t = jnp.broadcast_to(a_im, (1, t, N_GROUPS, STATE_DIM))

    def combine(e1, e2):
        a1r, a1i, b1r, b1i = e1
        a2r, a2i, b2r, b2i = e2
        return (a1r * a2r - a1i * a2i, a1r * a2i + a1i * a2r,
                a2r * b1r - a2i * b1i + b2r, a2r * b1i + a2i * b1r + b2i)

    _, _, hr, hi = lax.associative_scan(combine, (a_re_t, a_im_t, bu_re, bu_im), axis=1)
    y = (jnp.einsum('btgp,gnp->btgn', hr, c_re.astype(f32))
         - jnp.einsum('btgp,gnp->btgn', hi, c_im.astype(f32)))
    y = y.reshape(b, t, W_SSM) + d_skip.astype(f32) * u32.reshape(b, t, W_SSM)
    return y.astype(u.dtype), hr[:, -1], hi[:, -1]


def mixer_layer(x, k_past, v_past, h0_re, h0_im, layer_idx,
                norm_g, w_in, q_norm_g, k_norm_g, lam_q1, lam_k1, lam_q2, lam_k2, subln_g,
                w_attn_proj, lam_re, lam_im, log_dt, b_re, b_im, c_re, c_im, d_skip,
                w_glu, b_glu, w_ssm_proj, w_out):
    b, t, _ = x.shape
    f32 = jnp.float32
    xn = rmsnorm(x, norm_g)
    h = xn @ w_in
    q, k, v, ga, u, gs, gl = jnp.split(h, IN_SPLITS, axis=-1)
    q = q.reshape(b, t, 2, N_HEADS, HEAD_DIM)
    k = k.reshape(b, t, 2, N_HEADS, HEAD_DIM)
    v = v.reshape(b, t, N_HEADS, V_DIM)
    q = rmsnorm(q, q_norm_g).astype(f32) * (HEAD_DIM ** -0.5)
    k = rmsnorm(k, k_norm_g)
    lam_init = 0.8 - 0.6 * math.exp(-0.3 * layer_idx)
    lam = (jnp.exp(jnp.sum(lam_q1.astype(f32) * lam_k1.astype(f32)))
           - jnp.exp(jnp.sum(lam_q2.astype(f32) * lam_k2.astype(f32))) + lam_init)
    slopes = alibi_slopes()
    if k_past is None:
        o = attn_prompt(q, k, v, lam, slopes)
    else:
        o = attn_sample(q, k, v, k_past, v_past, lam, slopes)
    o = rmsnorm(o, subln_g) * (1.0 - lam_init)
    o = o.reshape(b, t, W_ATTN).astype(x.dtype) * jax.nn.silu(ga)
    ya = o @ w_attn_proj
    ys, hr, hi = s5_scan(u, h0_re, h0_im, lam_re, lam_im, log_dt, b_re, b_im, c_re, c_im, d_skip)
    ys = jax.nn.gelu(ys)
    ys = ys * jax.nn.sigmoid(ys @ w_glu + b_glu)
    ys = (ys * jax.nn.silu(gs)) @ w_ssm_proj
    g_a, g_s = jnp.split(gl, 2, axis=-1)
    merged = jax.nn.sigmoid(g_a) * ya + jax.nn.sigmoid(g_s) * ys
    x = x + merged @ w_out
    return x, k, v, hr, hi


def setup_inputs(seed: int = 0) -> dict:
    key = jax.random.key(seed)
    ks = jax.random.split(key, 32)
    f32 = jnp.float32
    nrm = lambda k, s, sc: sc * jax.random.normal(k, s, f32)
    lam_im = (jnp.pi * jnp.arange(STATE_DIM, dtype=f32))[None, None, :] + nrm(ks[11], (DEPTH, N_GROUPS, STATE_DIM), 0.01)
    return {
        "x_prompt": nrm(ks[0], (BATCH, SEQ, D_MODEL), 1.0),
        "x_sample": nrm(ks[1], (DEC_BATCH, DEC_SEQ, D_MODEL), 1.0),
        "cache_k": nrm(ks[2], (DEPTH, DEC_BATCH, PAST_LEN, 2, N_HEADS, HEAD_DIM), 1.0),
        "cache_v": nrm(ks[3], (DEPTH, DEC_BATCH, PAST_LEN, N_HEADS, V_DIM), 1.0),
        "state_ssm_re": nrm(ks[4], (DEPTH, DEC_BATCH, N_GROUPS, STATE_DIM), 0.5),
        "state_ssm_im": nrm(ks[5], (DEPTH, DEC_BATCH, N_GROUPS, STATE_DIM), 0.5),
        "norm_g": 1.0 + nrm(ks[6], (DEPTH, D_MODEL), 0.02),
        "w_in": nrm(ks[7], (DEPTH, D_MODEL, IN_COLS), D_MODEL ** -0.5),
        "q_norm_g": 1.0 + nrm(ks[8], (DEPTH, HEAD_DIM), 0.02),
        "k_norm_g": 1.0 + nrm(ks[9], (DEPTH, HEAD_DIM), 0.02),
        "lam_q1": nrm(ks[12], (DEPTH, HEAD_DIM), 0.1),
        "lam_k1": nrm(ks[13], (DEPTH, HEAD_DIM), 0.1),
        "lam_q2": nrm(ks[14], (DEPTH, HEAD_DIM), 0.1),
        "lam_k2": nrm(ks[15], (DEPTH, HEAD_DIM), 0.1),
        "subln_g": 1.0 + nrm(ks[16], (DEPTH, V_DIM), 0.02),
        "w_attn_proj": nrm(ks[17], (DEPTH, W_ATTN, D_MODEL), W_ATTN ** -0.5),
        "ssm_lambda_re": -0.5 + nrm(ks[10], (DEPTH, N_GROUPS, STATE_DIM), 0.01),
        "ssm_lambda_im": lam_im,
        "ssm_log_dt": jax.random.uniform(ks[18], (DEPTH, N_GROUPS), f32, math.log(1e-3), math.log(1e-1)),
        "ssm_b_re": nrm(ks[19], (DEPTH, N_GROUPS, STATE_DIM, GROUP_CH), (2.0 * GROUP_CH) ** -0.5),
        "ssm_b_im": nrm(ks[20], (DEPTH, N_GROUPS, STATE_DIM, GROUP_CH), (2.0 * GROUP_CH) ** -0.5),
        "ssm_c_re": nrm(ks[21], (DEPTH, N_GROUPS, GROUP_CH, STATE_DIM), (2.0 * STATE_DIM) ** -0.5),
        "ssm_c_im": nrm(ks[22], (DEPTH, N_GROUPS, GROUP_CH, STATE_DIM), (2.0 * STATE_DIM) ** -0.5),
        "ssm_d": nrm(ks[23], (DEPTH, W_SSM), 1.0),
        "w_glu": nrm(ks[24], (DEPTH, W_SSM, W_SSM), W_SSM ** -0.5),
        "b_glu": nrm(ks[25], (DEPTH, W_SSM), 0.02),
        "w_ssm_proj": nrm(ks[26], (DEPTH, W_SSM, D_MODEL), W_SSM ** -0.5),
        "w_out": nrm(ks[27], (DEPTH, D_MODEL, D_MODEL), D_MODEL ** -0.5),
    }


def reference(x_prompt, x_sample, cache_k, cache_v, state_ssm_re, state_ssm_im,
              norm_g, w_in, q_norm_g, k_norm_g, lam_q1, lam_k1, lam_q2, lam_k2, subln_g,
              w_attn_proj, ssm_lambda_re, ssm_lambda_im, ssm_log_dt, ssm_b_re, ssm_b_im,
              ssm_c_re, ssm_c_im, ssm_d, w_glu, b_glu, w_ssm_proj, w_out):
    yp, ys = x_prompt, x_sample
    kp_l, vp_l, hrp_l, hip_l = [], [], [], []
    ks_l, vs_l, hrs_l, his_l = [], [], [], []
    zeros_h = jnp.zeros((x_prompt.shape[0], N_GROUPS, STATE_DIM), jnp.float32)
    for l in range(DEPTH):
        lw = (norm_g[l], w_in[l], q_norm_g[l], k_norm_g[l], lam_q1[l], lam_k1[l], lam_q2[l], lam_k2[l],
              subln_g[l], w_attn_proj[l], ssm_lambda_re[l], ssm_lambda_im[l], ssm_log_dt[l],
              ssm_b_re[l], ssm_b_im[l], ssm_c_re[l], ssm_c_im[l], ssm_d[l], w_glu[l], b_glu[l],
              w_ssm_proj[l], w_out[l])
        yp, kp, vp, hrp, hip = mixer_layer(yp, None, None, zeros_h, zeros_h, l, *lw)
        ys, kn, vn, hrs, his = mixer_layer(ys, cache_k[l], cache_v[l], state_ssm_re[l], state_ssm_im[l], l, *lw)
        kp_l.append(kp); vp_l.append(vp); hrp_l.append(hrp); hip_l.append(hip)
        ks_l.append(kn); vs_l.append(vn); hrs_l.append(hrs); his_l.append(his)
    return (yp, ys,
            jnp.stack(kp_l), jnp.stack(vp_l), jnp.stack(hrp_l), jnp.stack(hip_l),
            jnp.stack(ks_l), jnp.stack(vs_l), jnp.stack(hrs_l), jnp.stack(his_l))
```

```python
import functools
import math

import jax
import jax.numpy as jnp
from jax import lax
from jax.experimental import pallas as pl
from jax.experimental.pallas import tpu as pltpu

D_MODEL = 1024
CHUNK = 64
N_HEADS = 8
HEAD_DIM = 64
V_DIM = 2 * HEAD_DIM
W_ATTN = N_HEADS * V_DIM
QK_COLS = 2 * N_HEADS * HEAD_DIM
GROUP_CH = 16
W_SSM = 1024
N_GROUPS = W_SSM // GROUP_CH
STATE_DIM = 64
EPS = 1e-6

LANES = 128
SUBLANES = 8
SLAB_GROUPS = LANES // GROUP_CH
N_SLABS = N_GROUPS // SLAB_GROUPS
SLAB_STATE = SLAB_GROUPS * STATE_DIM
NEG_BIG = -1e30
_LOG2_CHUNK = CHUNK.bit_length() - 1
_LOG2_HEAD_DIM = HEAD_DIM.bit_length() - 1

F32 = jnp.float32
BF16 = jnp.bfloat16


def _sigmoid(x):
    return 1.0 / (1.0 + jnp.exp(-x))


def _silu(x):
    return x * _sigmoid(x)


def _lam_value(lq1, lk1, lq2, lk2, lam_init):
    a = jnp.sum(lq1[...] * lk1[...], axis=-1, keepdims=True)
    b = jnp.sum(lq2[...] * lk2[...], axis=-1, keepdims=True)
    return jnp.exp(a) - jnp.exp(b) + lam_init


def _in_proj_kernel(x_ref, ng_ref, w_ref, gmean_ref, qg_ref, kg_ref,
                    q_ref, k_ref, v_ref, ga_ref, u_ref, gs_ref, gl_ref):
    x = x_ref[...]
    ms = jnp.mean(x * x, axis=-1, keepdims=True)
    xn = (x * lax.rsqrt(ms + EPS) * ng_ref[...]).astype(BF16)

    def section(i, n=1):
        return jnp.dot(xn, w_ref[:, i * 1024:(i + n) * 1024], preferred_element_type=F32)

    def head_norm(r, g_ref):
        msq = jnp.dot((r * r).astype(BF16), gmean_ref[...], preferred_element_type=F32)
        return r * lax.rsqrt(msq + EPS) * g_ref[...]

    q_ref[...] = (head_norm(section(0), qg_ref) * (HEAD_DIM ** -0.5)).astype(BF16)
    k_ref[...] = head_norm(section(1), kg_ref)
    v_ref[...] = section(2)
    ga_ref[...] = _silu(section(3)).astype(BF16)
    u_ref[...] = section(4)
    gs_ref[...] = _silu(section(5)).astype(BF16)
    gl_ref[...] = _sigmoid(section(6, 2)).astype(BF16)


def _in_proj(x, norm_g, w_in_bf, gmean, qg, kg, *, tm, time_major_u):
    B, T, D = x.shape
    grid = (B, T // tm)
    tok = lambda w: pl.BlockSpec((None, tm, w), lambda b, t: (b, t, 0))
    const = lambda shape: pl.BlockSpec(shape, lambda b, t: (0,) * len(shape))
    if time_major_u:
        u_shape = jax.ShapeDtypeStruct((T, B * W_SSM), F32)
        u_spec = pl.BlockSpec((tm, W_SSM), lambda b, t: (t, b))
    else:
        u_shape = jax.ShapeDtypeStruct((B, T, W_SSM), F32)
        u_spec = tok(W_SSM)
    out_shape = (
        jax.ShapeDtypeStruct((B, T, QK_COLS), BF16),
        jax.ShapeDtypeStruct((B, T, QK_COLS), F32),
        jax.ShapeDtypeStruct((B, T, W_ATTN), F32),
        jax.ShapeDtypeStruct((B, T, W_ATTN), BF16),
        u_shape,
        jax.ShapeDtypeStruct((B, T, W_SSM), BF16),
        jax.ShapeDtypeStruct((B, T, 2 * D_MODEL), BF16),
    )
    out_specs = (tok(QK_COLS), tok(QK_COLS), tok(W_ATTN), tok(W_ATTN), u_spec, tok(W_SSM),
                 tok(2 * D_MODEL))
    return pl.pallas_call(
        _in_proj_kernel,
        out_shape=out_shape,
        grid=grid,
        in_specs=[tok(D), const((1, D)), const(w_in_bf.shape), const(gmean.shape),
                  const((1, QK_COLS)), const((1, QK_COLS))],
        out_specs=out_specs,
        compiler_params=pltpu.CompilerParams(
            dimension_semantics=("parallel", "parallel"), vmem_limit_bytes=56 * 2 ** 20),
        name="in_proj",
    )(x, norm_g, w_in_bf, gmean, qg, kg)


def _flash_kernel(lam_init, tq, q0_ref, q1_ref, k0_ref, k1_ref, v_ref, ga_ref,
                  lq1, lk1, lq2, lk2, sg_ref, o_ref,
                  kb0, kb1, vb, m_sc, l_sc, acc_sc):
    T = q0_ref.shape[0]
    j = pl.program_id(1)
    kb0[...] = k0_ref[...].astype(BF16)
    kb1[...] = k1_ref[...].astype(BF16)
    vb[...] = v_ref[...].astype(BF16)
    lam = _lam_value(lq1, lk1, lq2, lk2, lam_init)
    lane = lax.broadcasted_iota(jnp.int32, (1, LANES), 1)
    row = lax.broadcasted_iota(jnp.int32, (tq, tq), 0)
    col = lax.broadcasted_iota(jnp.int32, (tq, tq), 1)
    nt = (((1,), (1,)), ((), ()))

    for hh in range(2):
        head = (2 * j + hh + 1).astype(F32)
        slope = jnp.exp2(-jnp.full((1, 1), head, F32))
        in_head = (lane >> _LOG2_HEAD_DIM) == hh
        vsl = slice(hh * V_DIM, (hh + 1) * V_DIM)

        def q_block(qb, carry):
            q_rows = pl.ds(pl.multiple_of(qb * tq, tq), tq)
            qs = (jnp.where(in_head, q0_ref[q_rows, :], 0).astype(BF16),
                  jnp.where(in_head, q1_ref[q_rows, :], 0).astype(BF16))
            m_sc[...] = jnp.full(m_sc.shape, NEG_BIG, F32)
            l_sc[...] = jnp.zeros(l_sc.shape, F32)
            acc_sc[...] = jnp.zeros(acc_sc.shape, F32)

            def kv_block(kb, c):
                k_rows = pl.ds(pl.multiple_of(kb * tq, tq), tq)
                t_pos = qb * tq + row
                s_pos = kb * tq + col
                bias = slope * jnp.abs(t_pos - s_pos).astype(F32)
                allowed = (s_pos >> _LOG2_CHUNK) <= (t_pos >> _LOG2_CHUNK)
                vv = vb[k_rows, vsl]
                for m, kref in enumerate((kb0, kb1)):
                    s = lax.dot_general(qs[m], kref[k_rows, :], nt, preferred_element_type=F32)
                    s = jnp.where(allowed, s - bias, NEG_BIG)
                    m_old = m_sc[m]
                    m_new = jnp.maximum(m_old, jnp.max(s, axis=-1, keepdims=True))
                    alpha = jnp.exp(m_old - m_new)
                    p = jnp.exp(s - m_new)
                    l_sc[m] = alpha * l_sc[m] + jnp.sum(p, axis=-1, keepdims=True)
                    acc_sc[m] = alpha * acc_sc[m] + jnp.dot(p.astype(BF16), vv,
                                                            preferred_element_type=F32)
                    m_sc[m] = m_new
                return c

            lax.fori_loop(0, qb + 1, kv_block, 0)
            o = acc_sc[0] / l_sc[0] - lam * (acc_sc[1] / l_sc[1])
            ms = jnp.mean(o * o, axis=-1, keepdims=True)
            o = o * lax.rsqrt(ms + EPS) * sg_ref[...] * (1.0 - lam_init)
            o_ref[q_rows, vsl] = (o * ga_ref[q_rows, vsl].astype(F32)).astype(BF16)
            return carry

        lax.fori_loop(0, T // tq, q_block, 0)


def _flash(q, k, v, ga, lq1, lk1, lq2, lk2, sg, *, lam_init, tq):
    B, T, _ = q.shape
    n_pairs = N_HEADS // 2
    qk0 = pl.BlockSpec((None, T, LANES), lambda b, j: (b, 0, j))
    qk1 = pl.BlockSpec((None, T, LANES), lambda b, j: (b, 0, n_pairs + j))
    pair = pl.BlockSpec((None, T, 2 * V_DIM), lambda b, j: (b, 0, j))
    const = lambda shape: pl.BlockSpec(shape, lambda b, j: (0,) * len(shape))
    return pl.pallas_call(
        functools.partial(_flash_kernel, lam_init, tq),
        out_shape=jax.ShapeDtypeStruct((B, T, W_ATTN), BF16),
        grid=(B, n_pairs),
        in_specs=[qk0, qk1, qk0, qk1, pair, pair] + [const((1, HEAD_DIM))] * 4
                 + [const((1, V_DIM))],
        out_specs=pair,
        scratch_shapes=[pltpu.VMEM((T, LANES), BF16), pltpu.VMEM((T, LANES), BF16),
                        pltpu.VMEM((T, 2 * V_DIM), BF16),
                        pltpu.VMEM((2, tq, 1), F32), pltpu.VMEM((2, tq, 1), F32),
                        pltpu.VMEM((2, tq, V_DIM), F32)],
        compiler_params=pltpu.CompilerParams(
            dimension_semantics=("parallel", "parallel"), vmem_limit_bytes=48 * 2 ** 20),
        name="flash",
    )(q, q, k, k, v, ga, lq1, lk1, lq2, lk2, sg)


def _dec_attn_kernel(lam_init, qbd_ref, ck_ref, cv_ref, kn_ref, vn_ref, ga_ref,
                     lq1, lk1, lq2, lk2, sg_ref, o_ref):
    P = ck_ref.shape[0]
    tn = kn_ref.shape[0]
    log2_tn = tn.bit_length() - 1
    half = N_HEADS * HEAD_DIM
    lam = _lam_value(lq1, lk1, lq2, lk2, lam_init)
    lane = lax.broadcasted_iota(jnp.int32, (1, LANES), 1)
    slope = jnp.exp2(-((lane >> log2_tn) + 1).astype(F32))
    q_pos = P + (lane & (tn - 1))
    pos_p = lax.broadcasted_iota(jnp.int32, (P, LANES), 0)
    pos_n = P + lax.broadcasted_iota(jnp.int32, (tn, LANES), 0)

    def masked(s, k_pos):
        s = s - slope * jnp.abs(q_pos - k_pos).astype(F32)
        return jnp.where((k_pos >> _LOG2_CHUNK) <= (q_pos >> _LOG2_CHUNK), s, NEG_BIG)

    probs = []
    for m in range(2):
        cols = slice(m * half, (m + 1) * half)
        qbd = qbd_ref[m]
        sp = masked(jnp.dot(ck_ref[:, cols].astype(BF16), qbd, preferred_element_type=F32), pos_p)
        sn = masked(jnp.dot(kn_ref[:, cols].astype(BF16), qbd, preferred_element_type=F32), pos_n)
        mx = jnp.maximum(jnp.max(sp, axis=0, keepdims=True), jnp.max(sn, axis=0, keepdims=True))
        pp = jnp.exp(sp - mx)
        pn = jnp.exp(sn - mx)
        l = jnp.sum(pp, axis=0, keepdims=True) + jnp.sum(pn, axis=0, keepdims=True)
        probs.append((pp / l, pn / l))
    wp = probs[0][0] - lam * probs[1][0]
    wn = probs[0][1] - lam * probs[1][1]
    ot = (jnp.dot(wp.T.astype(BF16), cv_ref[...].astype(BF16), preferred_element_type=F32)
          + jnp.dot(wn.T.astype(BF16), vn_ref[...].astype(BF16), preferred_element_type=F32))
    for h in range(N_HEADS):
        vsl = slice(h * V_DIM, (h + 1) * V_DIM)
        o = ot[h * tn:(h + 1) * tn, vsl]
        ms = jnp.mean(o * o, axis=-1, keepdims=True)
        o = o * lax.rsqrt(ms + EPS) * sg_ref[...] * (1.0 - lam_init)
        o_ref[:, vsl] = (o * ga_ref[:, vsl].astype(F32)).astype(BF16)


def _dec_attn(qbd, ck, cv, kn, vn, ga, lq1, lk1, lq2, lk2, sg, *, lam_init):
    Bs, P, _ = ck.shape
    tn = kn.shape[1]
    per_b = lambda shape: pl.BlockSpec((None,) + shape, lambda b: (b,) + (0,) * len(shape))
    const = lambda shape: pl.BlockSpec(shape, lambda b: (0,) * len(shape))
    return pl.pallas_call(
        functools.partial(_dec_attn_kernel, lam_init),
        out_shape=jax.ShapeDtypeStruct((Bs, tn, W_ATTN), BF16),
        grid=(Bs,),
        in_specs=[per_b((2, N_HEADS * HEAD_DIM, LANES)), per_b((P, QK_COLS)), per_b((P, W_ATTN)),
                  per_b((tn, QK_COLS)), per_b((tn, W_ATTN)), per_b((tn, W_ATTN))]
                 + [const((1, HEAD_DIM))] * 4 + [const((1, V_DIM))],
        out_specs=per_b((tn, W_ATTN)),
        compiler_params=pltpu.CompilerParams(
            dimension_semantics=("parallel",), vmem_limit_bytes=48 * 2 ** 20),
        name="dec_attn",
    )(qbd, ck, cv, kn, vn, ga, lq1, lk1, lq2, lk2, sg)


def _ssm_kernel(lc, nb, u_ref, bb_ref, cm_ref, are_ref, aim_ref, d_ref, h0r_ref, h0i_ref,
                y_ref, hr_ref, hi_ref, buf, st_re, st_im):
    c = pl.program_id(1)

    @pl.when(c == 0)
    def _():
        st_re[...] = h0r_ref[...]
        st_im[...] = h0i_ref[...]

    u = u_ref[...]
    buf[...] = jnp.dot(u.astype(BF16), bb_ref[...], preferred_element_type=F32)
    a_re = jnp.broadcast_to(are_ref[...], (nb, SLAB_STATE))
    a_im = jnp.broadcast_to(aim_ref[...], (nb, SLAB_STATE))
    re = slice(0, SLAB_STATE)
    im = slice(SLAB_STATE, 2 * SLAB_STATE)

    def step(t, carry):
        h_re, h_im = carry
        rows = pl.ds(pl.multiple_of(t * nb, nb), nb)
        n_re = a_re * h_re - a_im * h_im + buf[rows, re]
        n_im = a_re * h_im + a_im * h_re + buf[rows, im]
        buf[rows, re] = n_re
        buf[rows, im] = n_im
        return n_re, n_im

    h_re, h_im = lax.fori_loop(0, lc, step, (st_re[...], st_im[...]), unroll=8)
    st_re[...] = h_re
    st_im[...] = h_im
    y_ref[...] = (jnp.dot(buf[...].astype(BF16), cm_ref[...], preferred_element_type=F32)
                  + d_ref[...] * u)

    @pl.when(c == pl.num_programs(1) - 1)
    def _():
        hr_ref[...] = h_re
        hi_ref[...] = h_im


def _ssm(u_tm, h0_re, h0_im, bb, cm, a_re, a_im, d_skip, *, nb, lc):
    rows = u_tm.shape[0]
    T = rows // nb
    blk = lc * nb
    u_spec = pl.BlockSpec((blk, LANES), lambda j, c: (c, j))
    slab3 = lambda shape: pl.BlockSpec((None,) + shape, lambda j, c: (j, 0, 0))
    st_spec = pl.BlockSpec((nb, SLAB_STATE), lambda j, c: (0, j))
    st_shape = jax.ShapeDtypeStruct((nb, N_GROUPS * STATE_DIM), F32)
    return pl.pallas_call(
        functools.partial(_ssm_kernel, lc, nb),
        out_shape=(jax.ShapeDtypeStruct((rows, W_SSM), F32), st_shape, st_shape),
        grid=(N_SLABS, T // lc),
        in_specs=[u_spec, slab3((LANES, 2 * SLAB_STATE)), slab3((2 * SLAB_STATE, LANES)),
                  slab3((1, SLAB_STATE)), slab3((1, SLAB_STATE)),
                  pl.BlockSpec((1, LANES), lambda j, c: (0, j)), st_spec, st_spec],
        out_specs=(u_spec, st_spec, st_spec),
        scratch_shapes=[pltpu.VMEM((blk, 2 * SLAB_STATE), F32),
                        pltpu.VMEM((nb, SLAB_STATE), F32), pltpu.VMEM((nb, SLAB_STATE), F32)],
        compiler_params=pltpu.CompilerParams(
            dimension_semantics=("parallel", "arbitrary"), vmem_limit_bytes=48 * 2 ** 20),
        name="ssm",
    )(u_tm, bb, cm, a_re, a_im, d_skip, h0_re, h0_im)


def _ssm_params(lam_re, lam_im, log_dt, b_re, b_im, c_re, c_im):
    dt = jnp.exp(log_dt)[:, None]
    mag = jnp.exp(lam_re * dt)
    ang = lam_im * dt
    a_re, a_im = mag * jnp.cos(ang), mag * jnp.sin(ang)
    z_re, z_im = a_re - 1.0, a_im
    den = lam_re * lam_re + lam_im * lam_im
    w_re = (z_re * lam_re + z_im * lam_im) / den
    w_im = (z_im * lam_re - z_re * lam_im) / den
    bb_re = w_re[..., None] * b_re - w_im[..., None] * b_im
    bb_im = w_re[..., None] * b_im + w_im[..., None] * b_re
    eye = jnp.eye(SLAB_GROUPS, dtype=F32)

    def in_slab(bb):
        bb = bb.reshape(N_SLABS, SLAB_GROUPS, STATE_DIM, GROUP_CH)
        return jnp.einsum('jgpn,gh->jgnhp', bb, eye).reshape(N_SLABS, LANES, SLAB_STATE)

    def out_slab(cc):
        cc = cc.reshape(N_SLABS, SLAB_GROUPS, GROUP_CH, STATE_DIM)
        return jnp.einsum('jgnp,gh->jgphn', cc, eye).reshape(N_SLABS, SLAB_STATE, LANES)

    bb = jnp.concatenate([in_slab(bb_re), in_slab(bb_im)], axis=-1).astype(BF16)
    cm = jnp.concatenate([out_slab(c_re), -out_slab(c_im)], axis=1).astype(BF16)
    a_re = a_re.reshape(N_SLABS, 1, SLAB_STATE)
    a_im = a_im.reshape(N_SLABS, 1, SLAB_STATE)
    return bb, cm, a_re, a_im


def _out_proj_kernel(x_ref, oa_ref, y_ref, gs_ref, gl_ref, wap_ref, wglu_ref, bglu_ref,
                     wsp_ref, wout_ref, o_ref):
    ya = jnp.dot(oa_ref[...], wap_ref[...], preferred_element_type=F32)
    yg = jax.nn.gelu(y_ref[...])
    z = yg * _sigmoid(jnp.dot(yg.astype(BF16), wglu_ref[...], preferred_element_type=F32)
                      + bglu_ref[...])
    ys = jnp.dot((z * gs_ref[...].astype(F32)).astype(BF16), wsp_ref[...],
                 preferred_element_type=F32)
    merged = (gl_ref[:, :D_MODEL].astype(F32) * ya + gl_ref[:, D_MODEL:].astype(F32) * ys)
    o_ref[...] = x_ref[...] + jnp.dot(merged.astype(BF16), wout_ref[...],
                                      preferred_element_type=F32)


def _out_proj(x, oa, y, gs, gl, wap, wglu, bglu, wsp, wout, *, tm, time_major_y):
    B, T, D = x.shape
    tok = lambda w: pl.BlockSpec((None, tm, w), lambda b, t: (b, t, 0))
    const = lambda shape: pl.BlockSpec(shape, lambda b, t: (0,) * len(shape))
    y_spec = pl.BlockSpec((tm, W_SSM), lambda b, t: (t, b)) if time_major_y else tok(W_SSM)
    return pl.pallas_call(
        _out_proj_kernel,
        out_shape=jax.ShapeDtypeStruct((B, T, D), F32),
        grid=(B, T // tm),
        in_specs=[tok(D), tok(W_ATTN), y_spec, tok(W_SSM), tok(2 * D_MODEL),
                  const(wap.shape), const(wglu.shape), const((1, W_SSM)), const(wsp.shape),
                  const(wout.shape)],
        out_specs=tok(D),
        compiler_params=pltpu.CompilerParams(
            dimension_semantics=("parallel", "parallel"), vmem_limit_bytes=48 * 2 ** 20),
        name="out_proj",
    )(x, oa, y, gs, gl, wap, wglu, bglu, wsp, wout)


def _largest_tile(n, cap):
    t = min(n, cap)
    while n % t:
        t //= 2
    return t


def kernel(x_prompt, x_sample, cache_k, cache_v, state_ssm_re, state_ssm_im, norm_g, w_in, q_norm_g, k_norm_g, lam_q1, lam_k1, lam_q2, lam_k2, subln_g, w_attn_proj, ssm_lambda_re, ssm_lambda_im, ssm_log_dt, ssm_b_re, ssm_b_im, ssm_c_re, ssm_c_im, ssm_d, w_glu, b_glu, w_ssm_proj, w_out):
    B, T, D = x_prompt.shape
    Bs, Ts, _ = x_sample.shape
    depth = w_in.shape[0]
    P = cache_k.shape[2]
    assert D == D_MODEL and B % SUBLANES == 0 and Bs % SUBLANES == 0
    assert N_HEADS * Ts == LANES, "sample attention packs (head, frame) on the lane axis"

    head_of = jnp.arange(QK_COLS) // HEAD_DIM
    gmean = ((head_of[:, None] == head_of[None, :]).astype(F32) / HEAD_DIM).astype(BF16)
    eye_h = jnp.eye(N_HEADS, dtype=BF16)
    zeros_h = jnp.zeros((B, N_GROUPS * STATE_DIM), F32)
    tm_p = _largest_tile(T, 256)
    tq = _largest_tile(T, 256)
    lc_p = _largest_tile(T, 64)
    n_s = Bs * Ts
    tm_s = _largest_tile(n_s, 256)

    yp, ys = x_prompt, x_sample.reshape(1, n_s, D)
    outs = [[] for _ in range(8)]
    for l in range(depth):
        lam_init = 0.8 - 0.6 * math.exp(-0.3 * l)
        row = lambda a: a[l].reshape(1, -1)
        ng = row(norm_g)
        w_in_bf = w_in[l].astype(BF16)
        qg = jnp.tile(q_norm_g[l], QK_COLS // HEAD_DIM).reshape(1, -1)
        kg = jnp.tile(k_norm_g[l], QK_COLS // HEAD_DIM).reshape(1, -1)
        lam_rows = (row(lam_q1), row(lam_k1), row(lam_q2), row(lam_k2))
        sg = row(subln_g)
        bb, cm, a_re, a_im = _ssm_params(ssm_lambda_re[l], ssm_lambda_im[l], ssm_log_dt[l],
                                         ssm_b_re[l], ssm_b_im[l], ssm_c_re[l], ssm_c_im[l])
        d_skip = row(ssm_d)
        proj_w = (w_attn_proj[l].astype(BF16), w_glu[l].astype(BF16), row(b_glu),
                  w_ssm_proj[l].astype(BF16), w_out[l].astype(BF16))

        q, k, v, ga, u, gs, gl = _in_proj(yp, ng, w_in_bf, gmean, qg, kg, tm=tm_p,
                                          time_major_u=True)
        oa = _flash(q, k, v, ga, *lam_rows, sg, lam_init=lam_init, tq=tq)
        y_ssm, hr, hi = _ssm(u.reshape(T * B, W_SSM), zeros_h, zeros_h, bb, cm, a_re, a_im,
                             d_skip, nb=B, lc=lc_p)
        yp = _out_proj(yp, oa, y_ssm.reshape(T, B * W_SSM), gs, gl, *proj_w, tm=tm_p,
                       time_major_y=True)
        outs[0].append(k.reshape(B, T, 2, N_HEADS, HEAD_DIM))
        outs[1].append(v.reshape(B, T, N_HEADS, V_DIM))
        outs[2].append(hr.reshape(B, N_GROUPS, STATE_DIM))
        outs[3].append(hi.reshape(B, N_GROUPS, STATE_DIM))

        q, k, v, ga, u, gs, gl = _in_proj(ys, ng, w_in_bf, gmean, qg, kg, tm=tm_s,
                                          time_major_u=False)
        q5 = q.reshape(Bs, Ts, 2, N_HEADS, HEAD_DIM)
        qbd = jnp.einsum('btmhd,hg->bmhdgt', q5, eye_h).reshape(Bs, 2, N_HEADS * HEAD_DIM, LANES)
        oa = _dec_attn(qbd, cache_k[l].reshape(Bs, P, QK_COLS), cache_v[l].reshape(Bs, P, W_ATTN),
                       k.reshape(Bs, Ts, QK_COLS), v.reshape(Bs, Ts, W_ATTN),
                       ga.reshape(Bs, Ts, W_ATTN), *lam_rows, sg, lam_init=lam_init)
        u_tm = u.reshape(Bs, Ts, W_SSM).transpose(1, 0, 2).reshape(n_s, W_SSM)
        y_ssm, hr, hi = _ssm(u_tm, state_ssm_re[l].reshape(Bs, -1), state_ssm_im[l].reshape(Bs, -1),
                             bb, cm, a_re, a_im, d_skip, nb=Bs, lc=Ts)
        y_tok = y_ssm.reshape(Ts, Bs, W_SSM).transpose(1, 0, 2).reshape(1, n_s, W_SSM)
        ys = _out_proj(ys, oa.reshape(1, n_s, W_ATTN), y_tok, gs, gl, *proj_w, tm=tm_s,
                       time_major_y=False)
        outs[4].append(k.reshape(Bs, Ts, 2, N_HEADS, HEAD_DIM))
        outs[5].append(v.reshape(Bs, Ts, N_HEADS, V_DIM))
        outs[6].append(hr.reshape(Bs, N_GROUPS, STATE_DIM))
        outs[7].append(hi.reshape(Bs, N_GROUPS, STATE_DIM))

    stacked = [jnp.stack(o) for o in outs]
    return (yp, ys.reshape(Bs, Ts, D), *stacked)
```
